```python
import math
import jax, jax.numpy as jnp
from jax import lax
import numpy as np

D_MODEL = 1024
BATCH = 16
SEQ = 2048
DEPTH = 1

HEAD_DIM = 64
SB_HEADS = 8
DIL_PAIRS = ((128, 1), (512, 4), (2048, 16))
DIL_HEADS_PER_GROUP = 4
DIL_HEADS = DIL_HEADS_PER_GROUP * len(DIL_PAIRS)
SB_WIDTH = SB_HEADS * HEAD_DIM
DIL_WIDTH = DIL_HEADS * HEAD_DIM
DIL_OUT_WIDTH = DIL_HEADS_PER_GROUP * HEAD_DIM
IN_WIDTH = 3 * SB_WIDTH + 3 * DIL_WIDTH + 2 * D_MODEL
D_FF = ((8 * D_MODEL + 3 * 256 - 1) // (3 * 256)) * 256
Q_BLOCK = 128
RMS_EPS = 1e-6
ALIBI_MAX_BIAS = 8.0
SPLITS = tuple(int(c) for c in np.cumsum([SB_WIDTH, SB_WIDTH, SB_WIDTH, DIL_WIDTH, DIL_WIDTH, DIL_WIDTH, D_MODEL]))

kernel_name = "hybrid_stickbreak_dilated_gated"


def rms_norm(x, g):
    xf = x.astype(jnp.float32)
    y = xf * lax.rsqrt(jnp.mean(xf * xf, axis=-1, keepdims=True) + RMS_EPS) * g.astype(jnp.float32)
    return y.astype(x.dtype)


def alibi_slopes(n):
    return jnp.exp2(-ALIBI_MAX_BIAS * jnp.arange(1, n + 1, dtype=jnp.float32) / n)


def stick_breaking_attention(q, k, v):
    b, s, h, dh = q.shape
    nb = s // Q_BLOCK
    scale = 1.0 / math.sqrt(dh)
    qb = q.reshape(b, nb, Q_BLOCK, h, dh).transpose(1, 0, 3, 2, 4)
    kpos = jnp.arange(s)

    def block(args):
        q_blk, t0 = args
        z = jnp.einsum('bhqd,bkhd->bhqk', q_blk, k, preferred_element_type=jnp.float32) * scale
        tpos = t0 + jnp.arange(Q_BLOCK)
        causal = kpos[None, :] < tpos[:, None]
        log_keep = jnp.where(causal, jax.nn.log_sigmoid(-z), 0.0)
        log_after = lax.cumsum(log_keep, axis=3, reverse=True) - log_keep
        a = jnp.where(causal, jnp.exp(jax.nn.log_sigmoid(z) + log_after), 0.0)
        return jnp.einsum('bhqk,bkhd->bqhd', a.astype(v.dtype), v)

    out = lax.map(block, (qb, jnp.arange(nb) * Q_BLOCK))
    return out.transpose(1, 0, 2, 3, 4).reshape(b, s, h * dh)


def dilated_group_attention(q, k, v, window, dilation, slopes):
    b, s, h, dh = q.shape
    L = s // dilation
    w = window // dilation
    blk = w
    nb = -(-L // blk)
    lp = nb * blk

    def to_sub(t):
        t = t.reshape(b, L, dilation, h, dh).transpose(0, 2, 3, 1, 4)
        return jnp.pad(t, ((0, 0), (0, 0), (0, 0), (0, lp - L), (0, 0)))

    def band(t):
        t = jnp.pad(t, ((0, 0), (0, 0), (0, 0), (blk, 0), (0, 0))).reshape(b, dilation, h, nb + 1, blk, dh)
        return jnp.concatenate([t[:, :, :, :-1], t[:, :, :, 1:]], axis=4)

    qb = to_sub(q).reshape(b, dilation, h, nb, blk, dh)
    kb = band(to_sub(k))
    vb = band(to_sub(v))
    scores = jnp.einsum('brhnqd,brhnkd->brhnqk', qb, kb, preferred_element_type=jnp.float32) / math.sqrt(dh)
    qa = jnp.arange(blk)
    kc = jnp.arange(2 * blk)
    dist = blk + qa[:, None] - kc[None, :]
    key_idx = (jnp.arange(nb)[:, None] - 1) * blk + kc[None, :]
    valid = ((dist >= 0) & (dist <= w))[None, :, :] & (key_idx >= 0)[:, None, :]
    scores = scores - slopes[:, None, None, None] * (dist * dilation).astype(jnp.float32)
    scores = jnp.where(valid, scores, -jnp.inf)
    m = scores.max(-1)
    p = jnp.exp(scores - m[..., None])
    l = p.sum(-1)
    num = jnp.einsum('brhnqk,brhnkd->brhnqd', p, vb.astype(jnp.float32))

    def from_sub(t):
        t = t.reshape((b, dilation, h, lp) + t.shape[5:])[:, :, :, :L]
        t = jnp.moveaxis(t, 3, 1)
        return t.reshape((b, s, h) + t.shape[4:])

    return from_sub(num), from_sub(m), from_sub(l)


def dilated_mixture_attention(q, k, v):
    b, s, _, dh = q.shape
    slopes = alibi_slopes(DIL_HEADS)
    nums, ms, ls = [], [], []
    for g, (window, dilation) in enumerate(DIL_PAIRS):
        sl = slice(g * DIL_HEADS_PER_GROUP, (g + 1) * DIL_HEADS_PER_GROUP)
        n_g, m_g, l_g = dilated_group_attention(q[:, :, sl], k[:, :, sl], v[:, :, sl], window, dilation, slopes[sl])
        nums.append(n_g); ms.append(m_g); ls.append(l_g)
    m = jnp.stack(ms)
    wts = jnp.exp(m - m.max(0))
    den = (wts * jnp.stack(ls)).sum(0)
    num = (wts[..., None] * jnp.stack(nums)).sum(0)
    out = num / den[..., None]
    return out.reshape(b, s, DIL_OUT_WIDTH)


def setup_inputs(seed: int = 0) -> dict:
    key = jax.random.key(seed)
    ks = jax.random.split(key, 11)
    f32 = jnp.float32

    def nrm(k, shape, fan_in):
        return jax.random.normal(k, shape, f32) * (fan_in ** -0.5)

    return {
        "x": jax.random.normal(ks[0], (BATCH, SEQ, D_MODEL), f32),
        "norm_mix_g": 1.0 + 0.01 * jax.random.normal(ks[1], (DEPTH, D_MODEL), f32),
        "w_in": nrm(ks[2], (DEPTH, D_MODEL, IN_WIDTH), D_MODEL),
        "w_sb_up": nrm(ks[3], (DEPTH, SB_WIDTH, D_MODEL), SB_WIDTH),
        "w_dil_up": nrm(ks[4], (DEPTH, DIL_OUT_WIDTH, D_MODEL), DIL_OUT_WIDTH),
        "w_out": nrm(ks[5], (DEPTH, D_MODEL, D_MODEL), D_MODEL),
        "norm_ffn_g": 1.0 + 0.01 * jax.random.normal(ks[6], (DEPTH, D_MODEL), f32),
        "w_ffn_in": nrm(ks[7], (DEPTH, D_MODEL, 2 * D_FF), D_MODEL),
        "w_ffn_out": nrm(ks[8], (DEPTH, D_FF, D_MODEL), D_FF),
        "norm_final_g": 1.0 + 0.01 * jax.random.normal(ks[9], (D_MODEL,), f32),
    }


def reference(x, norm_mix_g, w_in, w_sb_up, w_dil_up, w_out, norm_ffn_g, w_ffn_in, w_ffn_out, norm_final_g):
    b, s, _ = x.shape
    for i in range(DEPTH):
        u = rms_norm(x, norm_mix_g[i])
        proj = u @ w_in[i]
        q_sb, k_sb, v_sb, q_dl, k_dl, v_dl, gate_sb, gate_dl = jnp.split(proj, SPLITS, axis=-1)
        heads_sb = lambda t: t.reshape(b, s, SB_HEADS, HEAD_DIM)
        heads_dl = lambda t: t.reshape(b, s, DIL_HEADS, HEAD_DIM)
        o_sb = stick_breaking_attention(heads_sb(q_sb), heads_sb(k_sb), heads_sb(v_sb))
        o_dl = dilated_mixture_attention(heads_dl(q_dl), heads_dl(k_dl), heads_dl(v_dl)).astype(x.dtype)
        y_sb = o_sb @ w_sb_up[i]
        y_dl = o_dl @ w_dil_up[i]
        merged = jax.nn.sigmoid(gate_sb) * y_sb + jax.nn.sigmoid(gate_dl) * y_dl
        x = x + merged @ w_out[i]
        u2 = rms_norm(x, norm_ffn_g[i])
        g_ff, up_ff = jnp.split(u2 @ w_ffn_in[i], 2, axis=-1)
        x = x + (jax.nn.silu(g_ff) * up_ff) @ w_ffn_out[i]
    return rms_norm(x, norm_final_g)
```

```python
import functools
import math

import jax
import jax.numpy as jnp
from jax import lax
from jax.experimental import pallas as pl
from jax.experimental.pallas import tpu as pltpu

D_MODEL = 1024
HEAD_DIM = 64
SB_HEADS = 8
SB_WIDTH = SB_HEADS * HEAD_DIM
DIL_PAIRS = ((128, 1), (512, 4), (2048, 16))
DIL_GROUP_HEADS = 4
DIL_GROUP_WIDTH = DIL_GROUP_HEADS * HEAD_DIM
DIL_HEADS = DIL_GROUP_HEADS * len(DIL_PAIRS)
DIL_WIDTH = DIL_HEADS * HEAD_DIM
DIL_WINDOW_STEPS = 128
D_FF = 2816
RMS_EPS = 1e-6
ALIBI_MAX_BIAS = 8.0
QK_SCALE = 1.0 / math.sqrt(HEAD_DIM)

LANES = 128
MXU_TILE = 256
VMEM_LIMIT_BYTES = 56 * 1024 * 1024

BF16 = jnp.bfloat16
F32 = jnp.float32

NT_DIMS = (((1,), (1,)), ((), ()))


def _dot(a, b):
    return jnp.dot(a, b, preferred_element_type=F32)


def _dot_nt(a, b):
    return lax.dot_general(a, b, NT_DIMS, preferred_element_type=F32)


def _rms_norm(x, g):
    return x * lax.rsqrt(jnp.mean(x * x, axis=-1, keepdims=True) + RMS_EPS) * g


IN_TM = 512
SB_COLS = 3 * SB_WIDTH
GRP_COLS = 3 * DIL_GROUP_WIDTH
GATE_COLS = 2 * D_MODEL
IN_WIDTH = SB_COLS + 3 * GRP_COLS + GATE_COLS


def _in_proj_kernel(x_ref, g_ref, w_ref, sb_ref, d0_ref, d1_ref, d2_ref, gate_ref, u_scr, r_scr):
    u_scr[...] = _rms_norm(x_ref[0], g_ref[...]).astype(BF16)
    tm = u_scr.shape[0]

    def chunk(col):
        return _dot(u_scr[...], w_ref[:, col:col + MXU_TILE])

    for c in range(SB_COLS // MXU_TILE):
        sb_ref[0, :, c * MXU_TILE:(c + 1) * MXU_TILE] = chunk(c * MXU_TILE).astype(BF16)
    base = SB_COLS
    for c in range(GRP_COLS // MXU_TILE):
        d0_ref[0, 0, :, c * MXU_TILE:(c + 1) * MXU_TILE] = chunk(base + c * MXU_TILE).astype(BF16)
    for d_ref, (_, dil) in ((d1_ref, DIL_PAIRS[1]), (d2_ref, DIL_PAIRS[2])):
        base += GRP_COLS
        for c in range(GRP_COLS // MXU_TILE):
            res = chunk(base + c * MXU_TILE)
            for s in range(MXU_TILE // LANES):
                r_scr[s] = res[:, s * LANES:(s + 1) * LANES]
            for r in range(dil):
                for s in range(MXU_TILE // LANES):
                    col = c * MXU_TILE + s * LANES
                    d_ref[0, r, :, col:col + LANES] = r_scr[s, pl.ds(r, tm // dil, stride=dil), :].astype(BF16)
    base += GRP_COLS
    for c in range(GATE_COLS // MXU_TILE):
        gate_ref[0, :, c * MXU_TILE:(c + 1) * MXU_TILE] = chunk(base + c * MXU_TILE).astype(BF16)


def _in_proj(x, g, w):
    b, s, d = x.shape
    tm = min(IN_TM, s)
    outs = [jax.ShapeDtypeStruct((b, s, SB_COLS), BF16)]
    out_specs = [pl.BlockSpec((1, tm, SB_COLS), lambda bi, i: (bi, i, 0))]
    for _, dil in DIL_PAIRS:
        outs.append(jax.ShapeDtypeStruct((b, dil, s // dil, GRP_COLS), BF16))
        out_specs.append(pl.BlockSpec((1, dil, tm // dil, GRP_COLS), lambda bi, i: (bi, 0, i, 0)))
    outs.append(jax.ShapeDtypeStruct((b, s, GATE_COLS), BF16))
    out_specs.append(pl.BlockSpec((1, tm, GATE_COLS), lambda bi, i: (bi, i, 0)))
    return pl.pallas_call(
        _in_proj_kernel,
        grid=(b, s // tm),
        in_specs=[
            pl.BlockSpec((1, tm, d), lambda bi, i: (bi, i, 0)),
            pl.BlockSpec((1, d), lambda bi, i: (0, 0)),
            pl.BlockSpec((d, IN_WIDTH), lambda bi, i: (0, 0), pipeline_mode=pl.Buffered(1)),
        ],
        out_specs=out_specs,
        out_shape=outs,
        scratch_shapes=[pltpu.VMEM((tm, d), BF16), pltpu.VMEM((MXU_TILE // LANES, tm, LANES), F32)],
        compiler_params=pltpu.CompilerParams(
            dimension_semantics=("arbitrary", "arbitrary"), vmem_limit_bytes=VMEM_LIMIT_BYTES),
        name="in_proj",
    )(x, g, w)


SB_BLK = 256


def _sb_kernel(q_ref, k_ref, v_ref, o_ref, vt_scr):
    s_len = q_ref.shape[1]
    nb = s_len // SB_BLK
    for c in range(nb):
        vt_scr[c] = v_ref[0, c * SB_BLK:(c + 1) * SB_BLK, :].astype(F32).T.astype(BF16)

    lane = lax.broadcasted_iota(jnp.int32, (SB_BLK, LANES), 1)
    key_i = lax.broadcasted_iota(jnp.int32, (SB_BLK, SB_BLK), 0)
    qry_i = lax.broadcasted_iota(jnp.int32, (SB_BLK, SB_BLK), 1)
    causal = key_i < qry_i
    row = lax.broadcasted_iota(jnp.int32, (SB_BLK, 2 * SB_BLK), 0)
    col = lax.broadcasted_iota(jnp.int32, (SB_BLK, 2 * SB_BLK), 1)
    col = jnp.where(col >= SB_BLK, col - SB_BLK, col)
    suffix = jnp.where(col > row, 1.0, 0.0).astype(BF16)

    def q_block(qi, _):
        q = q_ref[0, pl.ds(pl.multiple_of(qi * SB_BLK, SB_BLK), SB_BLK), :]
        q_heads = (jnp.where(lane < HEAD_DIM, q, jnp.zeros_like(q)),
                   jnp.where(lane >= HEAD_DIM, q, jnp.zeros_like(q)))

        def k_block(kb, state, diagonal):
            k_blk = k_ref[0, pl.ds(pl.multiple_of(kb * SB_BLK, SB_BLK), SB_BLK), :]
            vt_blk = vt_scr[kb]
            new_state = []
            for h in range(2):
                keep_sum, acc = state[h]
                z = _dot_nt(k_blk, q_heads[h])
                sp = jnp.maximum(z, 0.0) + jnp.log(1.0 + jnp.exp(-jnp.abs(z)))
                if diagonal:
                    sp = jnp.where(causal, sp, 0.0)
                hi = sp.astype(BF16)
                lo = (sp - hi.astype(F32)).astype(BF16)
                after = _dot(suffix, jnp.concatenate([hi, lo], axis=0))
                a = jnp.exp(z - sp - after - keep_sum)
                if diagonal:
                    a = jnp.where(causal, a, 0.0)
                acc = acc + _dot(vt_blk[h * HEAD_DIM:(h + 1) * HEAD_DIM, :], a.astype(BF16))
                keep_sum = keep_sum + after[0:1, :] + sp[0:1, :]
                new_state.append((keep_sum, acc))
            return tuple(new_state)

        zero = (jnp.zeros((1, SB_BLK), F32), jnp.zeros((HEAD_DIM, SB_BLK), F32))
        state = k_block(qi, (zero, zero), True)
        state = lax.fori_loop(0, qi, lambda i, st: k_block(qi - 1 - i, st, False), state)
        out_t = jnp.concatenate([state[0][1], state[1][1]], axis=0)
        o_ref[0, pl.ds(pl.multiple_of(qi * SB_BLK, SB_BLK), SB_BLK), :] = out_t.T.astype(BF16)
        return 0

    lax.fori_loop(0, nb, q_block, 0)


def _sb_attn(sb_qkv):
    b, s, _ = sb_qkv.shape
    pairs = SB_WIDTH // LANES
    return pl.pallas_call(
        _sb_kernel,
        grid=(b, pairs),
        in_specs=[
            pl.BlockSpec((1, s, LANES), lambda bi, p: (bi, 0, p)),
            pl.BlockSpec((1, s, LANES), lambda bi, p: (bi, 0, pairs + p)),
            pl.BlockSpec((1, s, LANES), lambda bi, p: (bi, 0, 2 * pairs + p)),
        ],
        out_specs=pl.BlockSpec((1, s, LANES), lambda bi, p: (bi, 0, p)),
        out_shape=jax.ShapeDtypeStruct((b, s, SB_WIDTH), BF16),
        scratch_shapes=[pltpu.VMEM((s // SB_BLK, LANES, SB_BLK), BF16)],
        compiler_params=pltpu.CompilerParams(
            dimension_semantics=("arbitrary", "arbitrary"), vmem_limit_bytes=VMEM_LIMIT_BYTES),
        name="sb_attn",
    )(sb_qkv, sb_qkv, sb_qkv)


DIL_BLK = DIL_WINDOW_STEPS


def _dil_block(q, k_blk, v_blk, q_idx0, k_idx0, dil, slopes):
    nq, nk = q.shape[0], k_blk.shape[0]
    lane_q = lax.broadcasted_iota(jnp.int32, (nq, DIL_GROUP_WIDTH), 1)
    lane_k = lax.broadcasted_iota(jnp.int32, (nk, DIL_GROUP_WIDTH), 1)
    dist = (lax.broadcasted_iota(jnp.int32, (nq, nk), 0) + q_idx0) - (lax.broadcasted_iota(jnp.int32, (nq, nk), 1) + k_idx0)
    valid = (dist >= 0) & (dist <= DIL_WINDOW_STEPS)
    token_dist = (dist * dil).astype(F32)
    probs, v_heads = [], []
    m_full = jnp.zeros((nq, DIL_GROUP_WIDTH), F32)
    l_full = jnp.zeros((nq, DIL_GROUP_WIDTH), F32)
    for j in range(DIL_GROUP_HEADS):
        in_head_q = (lane_q >= j * HEAD_DIM) & (lane_q < (j + 1) * HEAD_DIM)
        in_head_k = (lane_k >= j * HEAD_DIM) & (lane_k < (j + 1) * HEAD_DIM)
        s = _dot_nt(jnp.where(in_head_q, q, jnp.zeros_like(q)), k_blk) - slopes[j] * token_dist
        s = jnp.where(valid, s, -jnp.inf)
        m = jnp.max(s, axis=1, keepdims=True)
        p = jnp.exp(s - m)
        l = jnp.sum(p, axis=1, keepdims=True)
        probs.append(p.astype(BF16))
        v_heads.append(jnp.where(in_head_k, v_blk, jnp.zeros_like(v_blk)))
        m_full = jnp.where(in_head_q, m, m_full)
        l_full = jnp.where(in_head_q, l, l_full)
    num = _dot(jnp.concatenate(probs, axis=1), jnp.concatenate(v_heads, axis=0))
    return m_full, l_full, num


def _dil_kernel(slopes_ref, a0_ref, a1_ref, a2_ref, o_ref, m_scr, l_scr, n_scr):
    slabs = DIL_GROUP_WIDTH // LANES
    qs, ks, vs = (slice(i * DIL_GROUP_WIDTH, (i + 1) * DIL_GROUP_WIDTH) for i in range(3))

    def merge(rows, m_new, l_new, n_new, first):
        for s in range(slabs):
            cols = slice(s * LANES, (s + 1) * LANES)
            m_b, l_b, n_b = m_new[:, cols], l_new[:, cols], n_new[:, cols]
            if not first:
                m_a, l_a, n_a = m_scr[s, rows, :], l_scr[s, rows, :], n_scr[s, rows, :]
                m_max = jnp.maximum(m_a, m_b)
                w_a, w_b = jnp.exp(m_a - m_max), jnp.exp(m_b - m_max)
                m_b, l_b, n_b = m_max, w_a * l_a + w_b * l_b, w_a * n_a + w_b * n_b
            m_scr[s, rows, :] = m_b
            l_scr[s, rows, :] = l_b
            n_scr[s, rows, :] = n_b

    for g, (a_ref, (_, dil)) in enumerate(zip((a0_ref, a1_ref, a2_ref), DIL_PAIRS)):
        sub_len = a_ref.shape[2]
        nblk = sub_len // DIL_BLK
        slopes = [slopes_ref[g * DIL_GROUP_HEADS + j] for j in range(DIL_GROUP_HEADS)]

        def unit(u, _, a_ref=a_ref, dil=dil, nblk=nblk, slopes=slopes, first=(g == 0)):
            r, n = u // nblk, u % nblk
            q0 = pl.multiple_of(n * DIL_BLK, DIL_BLK)
            if nblk == 1:
                k0, nk = 0, DIL_BLK
            else:
                k0, nk = pl.multiple_of(jnp.maximum(n - 1, 0) * DIL_BLK, DIL_BLK), 2 * DIL_BLK
            q = a_ref[0, r, pl.ds(q0, DIL_BLK), qs]
            k_blk = a_ref[0, r, pl.ds(k0, nk), ks]
            v_blk = a_ref[0, r, pl.ds(k0, nk), vs]
            m_new, l_new, n_new = _dil_block(q, k_blk, v_blk, q0, k0, dil, slopes)
            if dil == 1:
                rows = pl.ds(q0, DIL_BLK)
            else:
                rows = pl.ds(r + q0 * dil, DIL_BLK, stride=dil)
            merge(rows, m_new, l_new, n_new, first)
            return 0

        lax.fori_loop(0, dil * nblk, unit, 0)

    for s in range(slabs):
        o_ref[0, :, s * LANES:(s + 1) * LANES] = (n_scr[s] / l_scr[s]).astype(BF16)


def _dil_attn(slopes, a0, a1, a2):
    b = a0.shape[0]
    s = a0.shape[1] * a0.shape[2]
    slabs = DIL_GROUP_WIDTH // LANES

    def group_spec(a):
        return pl.BlockSpec((1,) + a.shape[1:], lambda bi: (bi, 0, 0, 0))

    return pl.pallas_call(
        _dil_kernel,
        grid=(b,),
        in_specs=[pl.BlockSpec(memory_space=pltpu.SMEM), group_spec(a0), group_spec(a1), group_spec(a2)],
        out_specs=pl.BlockSpec((1, s, DIL_GROUP_WIDTH), lambda bi: (bi, 0, 0)),
        out_shape=jax.ShapeDtypeStruct((b, s, DIL_GROUP_WIDTH), BF16),
        scratch_shapes=[pltpu.VMEM((slabs, s, LANES), F32)] * 3,
        compiler_params=pltpu.CompilerParams(
            dimension_semantics=("arbitrary",), vmem_limit_bytes=VMEM_LIMIT_BYTES),
        name="dil_attn",
    )(slopes, a0, a1, a2)


POST_TM = 256
FF_CHUNK = 256


def _post_kernel(x_ref, osb_ref, odl_ref, gate_ref, wsb_ref, wdl_ref, wout_ref, g2_ref, wfi_ref, wfo_ref, gf_ref,
                 o_ref, u_scr, acc_scr):
    y_sb = _dot(osb_ref[0], wsb_ref[...])
    y_dl = _dot(odl_ref[0], wdl_ref[...])
    gate_sb = jax.nn.sigmoid(gate_ref[0, :, :D_MODEL].astype(F32))
    gate_dl = jax.nn.sigmoid(gate_ref[0, :, D_MODEL:].astype(F32))
    merged = (gate_sb * y_sb + gate_dl * y_dl).astype(BF16)
    x1 = x_ref[0] + _dot(merged, wout_ref[...])
    acc_scr[...] = x1
    u_scr[...] = _rms_norm(x1, g2_ref[...]).astype(BF16)
    for c in range(D_FF // FF_CHUNK):
        lo = c * FF_CHUNK
        g_ff = _dot(u_scr[...], wfi_ref[:, lo:lo + FF_CHUNK])
        up_ff = _dot(u_scr[...], wfi_ref[:, D_FF + lo:D_FF + lo + FF_CHUNK])
        h = (g_ff * jax.nn.sigmoid(g_ff) * up_ff).astype(BF16)
        acc_scr[...] += _dot(h, wfo_ref[lo:lo + FF_CHUNK, :])
    o_ref[0] = _rms_norm(acc_scr[...], gf_ref[...])


def _post(x, o_sb, o_dl, gates, w_sb, w_dl, w_out, g2, w_fi, w_fo, gf):
    b, s, d = x.shape
    tm = min(POST_TM, s)

    def tok(width):
        return pl.BlockSpec((1, tm, width), lambda bi, i: (bi, i, 0))

    def whole(a):
        return pl.BlockSpec(a.shape, lambda bi, i: (0,) * a.ndim, pipeline_mode=pl.Buffered(1))

    return pl.pallas_call(
        _post_kernel,
        grid=(b, s // tm),
        in_specs=[tok(d), tok(SB_WIDTH), tok(DIL_GROUP_WIDTH), tok(GATE_COLS),
                  whole(w_sb), whole(w_dl), whole(w_out), whole(g2), whole(w_fi), whole(w_fo), whole(gf)],
        out_specs=tok(d),
        out_shape=jax.ShapeDtypeStruct((b, s, d), x.dtype),
        scratch_shapes=[pltpu.VMEM((tm, d), BF16), pltpu.VMEM((tm, d), F32)],
        compiler_params=pltpu.CompilerParams(
            dimension_semantics=("arbitrary", "arbitrary"), vmem_limit_bytes=VMEM_LIMIT_BYTES),
        name="post",
    )(x, o_sb, o_dl, gates, w_sb, w_dl, w_out, g2, w_fi, w_fo, gf)


def _arrange_w_in(w_in):
    o = 0
    parts = {}
    for name, width in (("q_sb", SB_WIDTH), ("k_sb", SB_WIDTH), ("v_sb", SB_WIDTH),
                        ("q_dl", DIL_WIDTH), ("k_dl", DIL_WIDTH), ("v_dl", DIL_WIDTH), ("gates", GATE_COLS)):
        parts[name] = w_in[:, o:o + width]
        o += width
    cols = [parts["q_sb"] * QK_SCALE, parts["k_sb"], parts["v_sb"]]
    for g in range(len(DIL_PAIRS)):
        grp = slice(g * DIL_GROUP_WIDTH, (g + 1) * DIL_GROUP_WIDTH)
        cols += [parts["q_dl"][:, grp] * QK_SCALE, parts["k_dl"][:, grp], parts["v_dl"][:, grp]]
    cols.append(parts["gates"])
    return jnp.concatenate(cols, axis=1).astype(BF16)


def kernel(x, norm_mix_g, w_in, w_sb_up, w_dil_up, w_out, norm_ffn_g, w_ffn_in, w_ffn_out, norm_final_g):
    assert norm_mix_g.shape[0] == 1, "one layer"
    slopes = jnp.exp2(-ALIBI_MAX_BIAS * jnp.arange(1, DIL_HEADS + 1, dtype=F32) / DIL_HEADS)
    sb_qkv, a0, a1, a2, gates = _in_proj(x, norm_mix_g[0][None, :], _arrange_w_in(w_in[0]))
    o_sb = _sb_attn(sb_qkv)
    o_dl = _dil_attn(slopes, a0, a1, a2)
    return _post(x, o_sb, o_dl, gates,
                 w_sb_up[0].astype(BF16), w_dil_up[0].astype(BF16), w_out[0].astype(BF16),
                 norm_ffn_g[0][None, :], w_ffn_in[0].astype(BF16), w_ffn_out[0].astype(BF16),
                 norm_final_g[None, :])
```

```python
import functools
import math

import jax
import jax.numpy as jnp
from jax import lax
from jax.experimental import pallas as pl
from jax.experimental.pallas import tpu as pltpu

D_MODEL = 1024
HEAD_DIM = 64
SB_HEADS = 8
SB_WIDTH = SB_HEADS * HEAD_DIM
DIL_PAIRS = ((128, 1), (512, 4), (2048, 16))
DIL_GROUP_HEADS = 4
DIL_GROUP_WIDTH = DIL_GROUP_HEADS * HEAD_DIM
DIL_HEADS = DIL_GROUP_HEADS * len(DIL_PAIRS)
DIL_WIDTH = DIL_HEADS * HEAD_DIM
DIL_WINDOW_STEPS = 128
D_FF = 2816
RMS_EPS = 1e-6
ALIBI_MAX_BIAS = 8.0
QK_SCALE = 1.0 / math.sqrt(HEAD_DIM)

LANES = 128
MXU_TILE = 256
VMEM_LIMIT_BYTES = 56 * 1024 * 1024

BF16 = jnp.bfloat16
F32 = jnp.float32

NT_DIMS = (((1,), (1,)), ((), ()))


def _dot(a, b):
    return jnp.dot(a, b, preferred_element_type=F32)


def _dot_nt(a, b):
    return lax.dot_general(a, b, NT_DIMS, preferred_element_type=F32)


def _rms_norm(x, g):
    return x * lax.rsqrt(jnp.mean(x * x, axis=-1, keepdims=True) + RMS_EPS) * g


IN_TM = 512
SB_COLS = 3 * SB_WIDTH
GRP_COLS = 3 * DIL_GROUP_WIDTH
GATE_COLS = 2 * D_MODEL
IN_WIDTH = SB_COLS + 3 * GRP_COLS + GATE_COLS


def _in_proj_kernel(x_ref, g_ref, w_ref, sb_ref, d0_ref, d1_ref, d2_ref, gate_ref, u_scr, r_scr):
    u_scr[...] = _rms_norm(x_ref[0], g_ref[...]).astype(BF16)
    tm = u_scr.shape[0]

    def chunk(col):
        return _dot(u_scr[...], w_ref[:, col:col + MXU_TILE])

    for c in range(SB_COLS // MXU_TILE):
        sb_ref[0, :, c * MXU_TILE:(c + 1) * MXU_TILE] = chunk(c * MXU_TILE).astype(BF16)
    base = SB_COLS
    for c in range(GRP_COLS // MXU_TILE):
        d0_ref[0, 0, :, c * MXU_TILE:(c + 1) * MXU_TILE] = chunk(base + c * MXU_TILE).astype(BF16)
    for d_ref, (_, dil) in ((d1_ref, DIL_PAIRS[1]), (d2_ref, DIL_PAIRS[2])):
        base += GRP_COLS
        for c in range(GRP_COLS // MXU_TILE):
            res = chunk(base + c * MXU_TILE)
            for s in range(MXU_TILE // LANES):
                r_scr[s] = res[:, s * LANES:(s + 1) * LANES]
            for r in range(dil):
                for s in range(MXU_TILE // LANES):
                    col = c * MXU_TILE + s * LANES
                    d_ref[0, r, :, col:col + LANES] = r_scr[s, pl.ds(r, tm // dil, stride=dil), :].astype(BF16)
    base += GRP_COLS
    for c in range(GATE_COLS // MXU_TILE):
        gate_ref[0, :, c * MXU_TILE:(c + 1) * MXU_TILE] = chunk(base + c * MXU_TILE).astype(BF16)


def _in_proj(x, g, w):
    b, s, d = x.shape
    tm = min(IN_TM, s)
    outs = [jax.ShapeDtypeStruct((b, s, SB_COLS), BF16)]
    out_specs = [pl.BlockSpec((1, tm, SB_COLS), lambda bi, i: (bi, i, 0))]
    for _, dil in DIL_PAIRS:
        outs.append(jax.ShapeDtypeStruct((b, dil, s // dil, GRP_COLS), BF16))
        out_specs.append(pl.BlockSpec((1, dil, tm // dil, GRP_COLS), lambda bi, i: (bi, 0, i, 0)))
    outs.append(jax.ShapeDtypeStruct((b, s, GATE_COLS), BF16))
    out_specs.append(pl.BlockSpec((1, tm, GATE_COLS), lambda bi, i: (bi, i, 0)))
    return pl.pallas_call(
        _in_proj_kernel,
        grid=(b, s // tm),
        in_specs=[
            pl.BlockSpec((1, tm, d), lambda bi, i: (bi, i, 0)),
            pl.BlockSpec((1, d), lambda bi, i: (0, 0)),
            pl.BlockSpec((d, IN_WIDTH), lambda bi, i: (0, 0), pipeline_mode=pl.Buffered(1)),
        ],
        out_specs=out_specs,
        out_shape=outs,
        scratch_shapes=[pltpu.VMEM((tm, d), BF16), pltpu.VMEM((MXU_TILE // LANES, tm, LANES), F32)],
        compiler_params=pltpu.CompilerParams(
            dimension_semantics=("arbitrary", "arbitrary"), vmem_limit_bytes=VMEM_LIMIT_BYTES),
        name="in_proj",
    )(x, g, w)


SB_BLK = 256
SB_STEP_HEADS = 4
SB_STEP_WIDTH = SB_STEP_HEADS * HEAD_DIM
LOG2_E = math.log2(math.e)


def _sb_kernel(q_ref, k_ref, v_ref, o_ref, vt_scr):
    s_len = q_ref.shape[1]
    nb = s_len // SB_BLK
    for c in range(nb):
        vt_scr[c] = v_ref[0, c * SB_BLK:(c + 1) * SB_BLK, :].astype(F32).T.astype(BF16)

    lane = lax.broadcasted_iota(jnp.int32, (SB_BLK, SB_STEP_WIDTH), 1)
    key_i = lax.broadcasted_iota(jnp.int32, (SB_BLK, SB_BLK), 0)
    qry_i = lax.broadcasted_iota(jnp.int32, (SB_BLK, SB_BLK), 1)
    causal = key_i < qry_i
    row = lax.broadcasted_iota(jnp.int32, (SB_BLK, 2 * SB_BLK), 0)
    col = lax.broadcasted_iota(jnp.int32, (SB_BLK, 2 * SB_BLK), 1)
    col = jnp.where(col >= SB_BLK, col - SB_BLK, col)
    suffix = jnp.where(col >= row, 1.0, 0.0).astype(BF16)
    heads = range(SB_STEP_HEADS)

    def blk(i):
        return pl.ds(pl.multiple_of(i * SB_BLK, SB_BLK), SB_BLK)

    def run_chains(chains, state):
        z2 = [_dot_nt(k_ref[0, blk(kb), :], qh) * LOG2_E for qh, kb, _, _ in chains]
        split = []
        for z, (_, _, diagonal, _) in zip(z2, chains):
            s = jnp.maximum(z, 0.0) + jnp.log2(1.0 + jnp.exp2(-jnp.abs(z)))
            if diagonal:
                s = jnp.where(causal, s, 0.0)
            hi = s.astype(BF16)
            lo = (s - hi.astype(F32)).astype(BF16)
            split.append(jnp.concatenate([hi, lo], axis=0))
        tail = [_dot(suffix, sp) for sp in split]
        new_state = list(state)
        for z, t, (_, kb, diagonal, slot) in zip(z2, tail, chains):
            keep_sum, acc = state[slot]
            a = jnp.exp2(z - t - keep_sum)
            if diagonal:
                a = jnp.where(causal, a, 0.0)
            h = slot % SB_STEP_HEADS
            vt = vt_scr[kb, h * HEAD_DIM:(h + 1) * HEAD_DIM, :]
            new_state[slot] = (keep_sum + t[0:1, :], acc + _dot(vt, a.astype(BF16)))
        return tuple(new_state)

    def q_pair(m, _):
        q_heads = []
        for j in range(2):
            q = q_ref[0, blk(2 * m + j), :]
            q_heads += [jnp.where((lane >= h * HEAD_DIM) & (lane < (h + 1) * HEAD_DIM), q, jnp.zeros_like(q))
                        for h in heads]
        zero = (jnp.zeros((1, SB_BLK), F32), jnp.zeros((HEAD_DIM, SB_BLK), F32))
        state = (zero,) * (2 * SB_STEP_HEADS)
        state = run_chains([(q_heads[j * SB_STEP_HEADS + h], 2 * m + j, True, j * SB_STEP_HEADS + h)
                            for j in range(2) for h in heads], state)
        state = run_chains([(q_heads[SB_STEP_HEADS + h], 2 * m, False, SB_STEP_HEADS + h) for h in heads], state)

        def shared(i, st):
            kb = 2 * m - 1 - i
            return run_chains([(q_heads[c], kb, False, c) for c in range(2 * SB_STEP_HEADS)], st)

        state = lax.fori_loop(0, 2 * m, shared, state)
        for j in range(2):
            out_t = jnp.concatenate([acc for _, acc in state[j * SB_STEP_HEADS:(j + 1) * SB_STEP_HEADS]], axis=0)
            o_ref[0, blk(2 * m + j), :] = out_t.T.astype(BF16)
        return 0

    lax.fori_loop(0, nb // 2, q_pair, 0)


def _sb_attn(sb_qkv):
    b, s, _ = sb_qkv.shape
    steps = SB_WIDTH // SB_STEP_WIDTH
    return pl.pallas_call(
        _sb_kernel,
        grid=(b, steps),
        in_specs=[
            pl.BlockSpec((1, s, SB_STEP_WIDTH), lambda bi, p: (bi, 0, p)),
            pl.BlockSpec((1, s, SB_STEP_WIDTH), lambda bi, p: (bi, 0, steps + p)),
            pl.BlockSpec((1, s, SB_STEP_WIDTH), lambda bi, p: (bi, 0, 2 * steps + p)),
        ],
        out_specs=pl.BlockSpec((1, s, SB_STEP_WIDTH), lambda bi, p: (bi, 0, p)),
        out_shape=jax.ShapeDtypeStruct((b, s, SB_WIDTH), BF16),
        scratch_shapes=[pltpu.VMEM((s // SB_BLK, SB_STEP_WIDTH, SB_BLK), BF16)],
        compiler_params=pltpu.CompilerParams(
            dimension_semantics=("arbitrary", "arbitrary"), vmem_limit_bytes=VMEM_LIMIT_BYTES),
        name="sb_attn",
    )(sb_qkv, sb_qkv, sb_qkv)


DIL_BLK = DIL_WINDOW_STEPS


def _dil_block(q, k_blk, v_blk, q_idx0, k_idx0, dil, slopes):
    nq, nk = q.shape[0], k_blk.shape[0]
    lane_q = lax.broadcasted_iota(jnp.int32, (nq, DIL_GROUP_WIDTH), 1)
    lane_k = lax.broadcasted_iota(jnp.int32, (nk, DIL_GROUP_WIDTH), 1)
    dist = (lax.broadcasted_iota(jnp.int32, (nq, nk), 0) + q_idx0) - (lax.broadcasted_iota(jnp.int32, (nq, nk), 1) + k_idx0)
    valid = (dist >= 0) & (dist <= DIL_WINDOW_STEPS)
    token_dist = (dist * dil).astype(F32)
    probs, v_heads = [], []
    m_full = jnp.zeros((nq, DIL_GROUP_WIDTH), F32)
    l_full = jnp.zeros((nq, DIL_GROUP_WIDTH), F32)
    for j in range(DIL_GROUP_HEADS):
        in_head_q = (lane_q >= j * HEAD_DIM) & (lane_q < (j + 1) * HEAD_DIM)
        in_head_k = (lane_k >= j * HEAD_DIM) & (lane_k < (j + 1) * HEAD_DIM)
        s = _dot_nt(jnp.where(in_head_q, q, jnp.zeros_like(q)), k_blk) - slopes[j] * token_dist
        s = jnp.where(valid, s, -jnp.inf)
        m = jnp.max(s, axis=1, keepdims=True)
        p = jnp.exp(s - m)
        l = jnp.sum(p, axis=1, keepdims=True)
        probs.append(p.astype(BF16))
        v_heads.append(jnp.where(in_head_k, v_blk, jnp.zeros_like(v_blk)))
        m_full = jnp.where(in_head_q, m, m_full)
        l_full = jnp.where(in_head_q, l, l_full)
    num = _dot(jnp.concatenate(probs, axis=1), jnp.concatenate(v_heads, axis=0))
    return m_full, l_full, num


def _dil_kernel(slopes_ref, a0_ref, a1_ref, a2_ref, o_ref, m_scr, l_scr, n_scr):
    slabs = DIL_GROUP_WIDTH // LANES
    qs, ks, vs = (slice(i * DIL_GROUP_WIDTH, (i + 1) * DIL_GROUP_WIDTH) for i in range(3))

    def merge(rows, m_new, l_new, n_new, first):
        for s in range(slabs):
            cols = slice(s * LANES, (s + 1) * LANES)
            m_b, l_b, n_b = m_new[:, cols], l_new[:, cols], n_new[:, cols]
            if not first:
                m_a, l_a, n_a = m_scr[s, rows, :], l_scr[s, rows, :], n_scr[s, rows, :]
                m_max = jnp.maximum(m_a, m_b)
                w_a, w_b = jnp.exp(m_a - m_max), jnp.exp(m_b - m_max)
                m_b, l_b, n_b = m_max, w_a * l_a + w_b * l_b, w_a * n_a + w_b * n_b
            m_scr[s, rows, :] = m_b
            l_scr[s, rows, :] = l_b
            n_scr[s, rows, :] = n_b

    for g, (a_ref, (_, dil)) in enumerate(zip((a0_ref, a1_ref, a2_ref), DIL_PAIRS)):
        sub_len = a_ref.shape[2]
        nblk = sub_len // DIL_BLK
        slopes = [slopes_ref[g * DIL_GROUP_HEADS + j] for j in range(DIL_GROUP_HEADS)]

        def unit(u, _, a_ref=a_ref, dil=dil, nblk=nblk, slopes=slopes, first=(g == 0)):
            r, n = u // nblk, u % nblk
            q0 = pl.multiple_of(n * DIL_BLK, DIL_BLK)
            if nblk == 1:
                k0, nk = 0, DIL_BLK
            else:
                k0, nk = pl.multiple_of(jnp.maximum(n - 1, 0) * DIL_BLK, DIL_BLK), 2 * DIL_BLK
            q = a_ref[0, r, pl.ds(q0, DIL_BLK), qs]
            k_blk = a_ref[0, r, pl.ds(k0, nk), ks]
            v_blk = a_ref[0, r, pl.ds(k0, nk), vs]
            m_new, l_new, n_new = _dil_block(q, k_blk, v_blk, q0, k0, dil, slopes)
            if dil == 1:
                rows = pl.ds(q0, DIL_BLK)
            else:
                rows = pl.ds(r + q0 * dil, DIL_BLK, stride=dil)
            merge(rows, m_new, l_new, n_new, first)
            return 0

        lax.fori_loop(0, dil * nblk, unit, 0)

    for s in range(slabs):
        o_ref[0, :, s * LANES:(s + 1) * LANES] = (n_scr[s] / l_scr[s]).astype(BF16)


def _dil_attn(slopes, a0, a1, a2):
    b = a0.shape[0]
    s = a0.shape[1] * a0.shape[2]
    slabs = DIL_GROUP_WIDTH // LANES

    def group_spec(a):
        return pl.BlockSpec((1,) + a.shape[1:], lambda bi: (bi, 0, 0, 0))

    return pl.pallas_call(
        _dil_kernel,
        grid=(b,),
        in_specs=[pl.BlockSpec(memory_space=pltpu.SMEM), group_spec(a0), group_spec(a1), group_spec(a2)],
        out_specs=pl.BlockSpec((1, s, DIL_GROUP_WIDTH), lambda bi: (bi, 0, 0)),
        out_shape=jax.ShapeDtypeStruct((b, s, DIL_GROUP_WIDTH), BF16),
        scratch_shapes=[pltpu.VMEM((slabs, s, LANES), F32)] * 3,
        compiler_params=pltpu.CompilerParams(
            dimension_semantics=("arbitrary",), vmem_limit_bytes=VMEM_LIMIT_BYTES),
        name="dil_attn",
    )(slopes, a0, a1, a2)


POST_TM = 256
FF_CHUNK = 256


def _post_kernel(x_ref, osb_ref, odl_ref, gate_ref, wsb_ref, wdl_ref, wout_ref, g2_ref, wfi_ref, wfo_ref, gf_ref,
                 o_ref, u_scr, acc_scr):
    y_sb = _dot(osb_ref[0], wsb_ref[...])
    y_dl = _dot(odl_ref[0], wdl_ref[...])
    gate_sb = jax.nn.sigmoid(gate_ref[0, :, :D_MODEL].astype(F32))
    gate_dl = jax.nn.sigmoid(gate_ref[0, :, D_MODEL:].astype(F32))
    merged = (gate_sb * y_sb + gate_dl * y_dl).astype(BF16)
    x1 = x_ref[0] + _dot(merged, wout_ref[...])
    acc_scr[...] = x1
    u_scr[...] = _rms_norm(x1, g2_ref[...]).astype(BF16)
    for c in range(D_FF // FF_CHUNK):
        lo = c * FF_CHUNK
        g_ff = _dot(u_scr[...], wfi_ref[:, lo:lo + FF_CHUNK])
        up_ff = _dot(u_scr[...], wfi_ref[:, D_FF + lo:D_FF + lo + FF_CHUNK])
        h = (g_ff * jax.nn.sigmoid(g_ff) * up_ff).astype(BF16)
        acc_scr[...] += _dot(h, wfo_ref[lo:lo + FF_CHUNK, :])
    o_ref[0] = _rms_norm(acc_scr[...], gf_ref[...])


def _post(x, o_sb, o_dl, gates, w_sb, w_dl, w_out, g2, w_fi, w_fo, gf):
    b, s, d = x.shape
    tm = min(POST_TM, s)

    def tok(width):
        return pl.BlockSpec((1, tm, width), lambda bi, i: (bi, i, 0))

    def whole(a):
        return pl.BlockSpec(a.shape, lambda bi, i: (0,) * a.ndim, pipeline_mode=pl.Buffered(1))

    return pl.pallas_call(
        _post_kernel,
        grid=(b, s // tm),
        in_specs=[tok(d), tok(SB_WIDTH), tok(DIL_GROUP_WIDTH), tok(GATE_COLS),
                  whole(w_sb), whole(w_dl), whole(w_out), whole(g2), whole(w_fi), whole(w_fo), whole(gf)],
        out_specs=tok(d),
        out_shape=jax.ShapeDtypeStruct((b, s, d), x.dtype),
        scratch_shapes=[pltpu.VMEM((tm, d), BF16), pltpu.VMEM((tm, d), F32)],
        compiler_params=pltpu.CompilerParams(
            dimension_semantics=("arbitrary", "arbitrary"), vmem_limit_bytes=VMEM_LIMIT_BYTES),
        name="post",
    )(x, o_sb, o_dl, gates, w_sb, w_dl, w_out, g2, w_fi, w_fo, gf)


def _arrange_w_in(w_in):
    o = 0
    parts = {}
    for name, width in (("q_sb", SB_WIDTH), ("k_sb", SB_WIDTH), ("v_sb", SB_WIDTH),
                        ("q_dl", DIL_WIDTH), ("k_dl", DIL_WIDTH), ("v_dl", DIL_WIDTH), ("gates", GATE_COLS)):
        parts[name] = w_in[:, o:o + width]
        o += width
    cols = [parts["q_sb"] * QK_SCALE, parts["k_sb"], parts["v_sb"]]
    for g in range(len(DIL_PAIRS)):
        grp = slice(g * DIL_GROUP_WIDTH, (g + 1) * DIL_GROUP_WIDTH)
        cols += [parts["q_dl"][:, grp] * QK_SCALE, parts["k_dl"][:, grp], parts["v_dl"][:, grp]]
    cols.append(parts["gates"])
    return jnp.concatenate(cols, axis=1).astype(BF16)


def kernel(x, norm_mix_g, w_in, w_sb_up, w_dil_up, w_out, norm_ffn_g, w_ffn_in, w_ffn_out, norm_final_g):
    assert norm_mix_g.shape[0] == 1, "one layer"
    slopes = jnp.exp2(-ALIBI_MAX_BIAS * jnp.arange(1, DIL_HEADS + 1, dtype=F32) / DIL_HEADS)
    sb_qkv, a0, a1, a2, gates = _in_proj(x, norm_mix_g[0][None, :], _arrange_w_in(w_in[0]))
    o_sb = _sb_attn(sb_qkv)
    o_dl = _dil_attn(slopes, a0, a1, a2)
    return _post(x, o_sb, o_dl, gates,
                 w_sb_up[0].astype(BF16), w_dil_up[0].astype(BF16), w_out[0].astype(BF16),
                 norm_ffn_g[0][None, :], w_ffn_in[0].astype(BF16), w_ffn_out[0].astype(BF16),
                 norm_final_g[None, :])
```

```python
import functools
import math

import jax
import jax.numpy as jnp
from jax import lax
from jax.experimental import pallas as pl
from jax.experimental.pallas import tpu as pltpu

D_MODEL = 1024
HEAD_DIM = 64
SB_HEADS = 8
SB_WIDTH = SB_HEADS * HEAD_DIM
DIL_PAIRS = ((128, 1), (512, 4), (2048, 16))
DIL_GROUP_HEADS = 4
DIL_GROUP_WIDTH = DIL_GROUP_HEADS * HEAD_DIM
DIL_HEADS = DIL_GROUP_HEADS * len(DIL_PAIRS)
DIL_WIDTH = DIL_HEADS * HEAD_DIM
DIL_WINDOW_STEPS = 128
D_FF = 2816
RMS_EPS = 1e-6
ALIBI_MAX_BIAS = 8.0
QK_SCALE = 1.0 / math.sqrt(HEAD_DIM)

LANES = 128
MXU_TILE = 256
VMEM_LIMIT_BYTES = 56 * 1024 * 1024

BF16 = jnp.bfloat16
F32 = jnp.float32

NT_DIMS = (((1,), (1,)), ((), ()))


def _dot(a, b):
    return jnp.dot(a, b, preferred_element_type=F32)


def _dot_nt(a, b):
    return lax.dot_general(a, b, NT_DIMS, preferred_element_type=F32)


def _rms_norm(x, g):
    return x * lax.rsqrt(jnp.mean(x * x, axis=-1, keepdims=True) + RMS_EPS) * g


IN_TM = 512
SB_COLS = 3 * SB_WIDTH
GRP_COLS = 3 * DIL_GROUP_WIDTH
GATE_COLS = 2 * D_MODEL
IN_WIDTH = SB_COLS + 3 * GRP_COLS + GATE_COLS


def _in_proj_kernel(x_ref, g_ref, w_ref, sb_ref, d0_ref, d1_ref, d2_ref, gate_ref, u_scr, r_scr):
    u_scr[...] = _rms_norm(x_ref[0], g_ref[...]).astype(BF16)
    tm = u_scr.shape[0]

    def chunk(col):
        return _dot(u_scr[...], w_ref[:, col:col + MXU_TILE])

    for c in range(SB_COLS // MXU_TILE):
        sb_ref[0, :, c * MXU_TILE:(c + 1) * MXU_TILE] = chunk(c * MXU_TILE).astype(BF16)
    base = SB_COLS
    for c in range(GRP_COLS // MXU_TILE):
        d0_ref[0, 0, :, c * MXU_TILE:(c + 1) * MXU_TILE] = chunk(base + c * MXU_TILE).astype(BF16)
    for d_ref, (_, dil) in ((d1_ref, DIL_PAIRS[1]), (d2_ref, DIL_PAIRS[2])):
        base += GRP_COLS
        for c in range(GRP_COLS // MXU_TILE):
            res = chunk(base + c * MXU_TILE)
            for s in range(MXU_TILE // LANES):
                r_scr[s] = res[:, s * LANES:(s + 1) * LANES]
            for r in range(dil):
                for s in range(MXU_TILE // LANES):
                    col = c * MXU_TILE + s * LANES
                    d_ref[0, r, :, col:col + LANES] = r_scr[s, pl.ds(r, tm // dil, stride=dil), :].astype(BF16)
    base += GRP_COLS
    for c in range(GATE_COLS // MXU_TILE):
        gate_ref[0, :, c * MXU_TILE:(c + 1) * MXU_TILE] = chunk(base + c * MXU_TILE).astype(BF16)


def _in_proj(x, g, w):
    b, s, d = x.shape
    tm = min(IN_TM, s)
    outs = [jax.ShapeDtypeStruct((b, s, SB_COLS), BF16)]
    out_specs = [pl.BlockSpec((1, tm, SB_COLS), lambda bi, i: (bi, i, 0))]
    for _, dil in DIL_PAIRS:
        outs.append(jax.ShapeDtypeStruct((b, dil, s // dil, GRP_COLS), BF16))
        out_specs.append(pl.BlockSpec((1, dil, tm // dil, GRP_COLS), lambda bi, i: (bi, 0, i, 0)))
    outs.append(jax.ShapeDtypeStruct((b, s, GATE_COLS), BF16))
    out_specs.append(pl.BlockSpec((1, tm, GATE_COLS), lambda bi, i: (bi, i, 0)))
    return pl.pallas_call(
        _in_proj_kernel,
        grid=(b, s // tm),
        in_specs=[
            pl.BlockSpec((1, tm, d), lambda bi, i: (bi, i, 0)),
            pl.BlockSpec((1, d), lambda bi, i: (0, 0)),
            pl.BlockSpec((d, IN_WIDTH), lambda bi, i: (0, 0), pipeline_mode=pl.Buffered(1)),
        ],
        out_specs=out_specs,
        out_shape=outs,
        scratch_shapes=[pltpu.VMEM((tm, d), BF16), pltpu.VMEM((MXU_TILE // LANES, tm, LANES), F32)],
        compiler_params=pltpu.CompilerParams(
            dimension_semantics=("arbitrary", "arbitrary"), vmem_limit_bytes=VMEM_LIMIT_BYTES),
        name="in_proj",
    )(x, g, w)


SB_BLK = 256
SB_STEP_HEADS = 4
SB_STEP_WIDTH = SB_STEP_HEADS * HEAD_DIM
LOG2_E = math.log2(math.e)


def _sb_kernel(q_ref, k_ref, v_ref, o_ref, vt_scr):
    s_len = q_ref.shape[1]
    nb = s_len // SB_BLK
    for c in range(nb):
        vt_scr[c] = v_ref[0, c * SB_BLK:(c + 1) * SB_BLK, :].astype(F32).T.astype(BF16)

    lane = lax.broadcasted_iota(jnp.int32, (SB_BLK, SB_STEP_WIDTH), 1)
    key_i = lax.broadcasted_iota(jnp.int32, (SB_BLK, SB_BLK), 0)
    qry_i = lax.broadcasted_iota(jnp.int32, (SB_BLK, SB_BLK), 1)
    causal = key_i < qry_i
    suffix = jnp.where(qry_i >= key_i, 1.0, 0.0).astype(BF16)
    heads = range(SB_STEP_HEADS)

    def blk(i):
        return pl.ds(pl.multiple_of(i * SB_BLK, SB_BLK), SB_BLK)

    def run_chains(chains, state):
        z2 = [_dot_nt(k_ref[0, blk(kb), :], qh) * LOG2_E for qh, kb, _, _ in chains]
        split = []
        for z, (_, _, diagonal, _) in zip(z2, chains):
            s = jnp.maximum(z, 0.0) + jnp.log2(1.0 + jnp.exp2(-jnp.abs(z)))
            if diagonal:
                s = jnp.where(causal, s, 0.0)
            split.append(s.astype(BF16))
        tail = [_dot(suffix, sp) for sp in split]
        new_state = list(state)
        for z, t, (_, kb, diagonal, slot) in zip(z2, tail, chains):
            keep_sum, acc = state[slot]
            a = jnp.exp2(z - t - keep_sum)
            if diagonal:
                a = jnp.where(causal, a, 0.0)
            h = slot % SB_STEP_HEADS
            vt = vt_scr[kb, h * HEAD_DIM:(h + 1) * HEAD_DIM, :]
            new_state[slot] = (keep_sum + t[0:1, :], acc + _dot(vt, a.astype(BF16)))
        return tuple(new_state)

    def q_pair(m, _):
        q_heads = []
        for j in range(2):
            q = q_ref[0, blk(2 * m + j), :]
            q_heads += [jnp.where((lane >= h * HEAD_DIM) & (lane < (h + 1) * HEAD_DIM), q, jnp.zeros_like(q))
                        for h in heads]
        zero = (jnp.zeros((1, SB_BLK), F32), jnp.zeros((HEAD_DIM, SB_BLK), F32))
        state = (zero,) * (2 * SB_STEP_HEADS)
        state = run_chains([(q_heads[j * SB_STEP_HEADS + h], 2 * m + j, True, j * SB_STEP_HEADS + h)
                            for j in range(2) for h in heads], state)
        state = run_chains([(q_heads[SB_STEP_HEADS + h], 2 * m, False, SB_STEP_HEADS + h) for h in heads], state)

        def shared(i, st):
            kb = 2 * m - 1 - i
            return run_chains([(q_heads[c], kb, False, c) for c in range(2 * SB_STEP_HEADS)], st)

        state = lax.fori_loop(0, 2 * m, shared, state)
        for j in range(2):
            out_t = jnp.concatenate([acc for _, acc in state[j * SB_STEP_HEADS:(j + 1) * SB_STEP_HEADS]], axis=0)
            o_ref[0, blk(2 * m + j), :] = out_t.T.astype(BF16)
        return 0

    lax.fori_loop(0, nb // 2, q_pair, 0)


def _sb_attn(sb_qkv):
    b, s, _ = sb_qkv.shape
    steps = SB_WIDTH // SB_STEP_WIDTH
    return pl.pallas_call(
        _sb_kernel,
        grid=(b, steps),
        in_specs=[
            pl.BlockSpec((1, s, SB_STEP_WIDTH), lambda bi, p: (bi, 0, p)),
            pl.BlockSpec((1, s, SB_STEP_WIDTH), lambda bi, p: (bi, 0, steps + p)),
            pl.BlockSpec((1, s, SB_STEP_WIDTH), lambda bi, p: (bi, 0, 2 * steps + p)),
        ],
        out_specs=pl.BlockSpec((1, s, SB_STEP_WIDTH), lambda bi, p: (bi, 0, p)),
        out_shape=jax.ShapeDtypeStruct((b, s, SB_WIDTH), BF16),
        scratch_shapes=[pltpu.VMEM((s // SB_BLK, SB_STEP_WIDTH, SB_BLK), BF16)],
        compiler_params=pltpu.CompilerParams(
            dimension_semantics=("arbitrary", "arbitrary"), vmem_limit_bytes=VMEM_LIMIT_BYTES),
        name="sb_attn",
    )(sb_qkv, sb_qkv, sb_qkv)


DIL_BLK = DIL_WINDOW_STEPS


def _dil_block(q, k_blk, v_blk, q_idx0, k_idx0, dil, slopes):
    nq, nk = q.shape[0], k_blk.shape[0]
    lane_q = lax.broadcasted_iota(jnp.int32, (nq, DIL_GROUP_WIDTH), 1)
    lane_k = lax.broadcasted_iota(jnp.int32, (nk, DIL_GROUP_WIDTH), 1)
    dist = (lax.broadcasted_iota(jnp.int32, (nq, nk), 0) + q_idx0) - (lax.broadcasted_iota(jnp.int32, (nq, nk), 1) + k_idx0)
    valid = (dist >= 0) & (dist <= DIL_WINDOW_STEPS)
    token_dist = (dist * dil).astype(F32)
    probs, v_heads = [], []
    m_full = jnp.zeros((nq, DIL_GROUP_WIDTH), F32)
    l_full = jnp.zeros((nq, DIL_GROUP_WIDTH), F32)
    for j in range(DIL_GROUP_HEADS):
        in_head_q = (lane_q >= j * HEAD_DIM) & (lane_q < (j + 1) * HEAD_DIM)
        in_head_k = (lane_k >= j * HEAD_DIM) & (lane_k < (j + 1) * HEAD_DIM)
        s = _dot_nt(jnp.where(in_head_q, q, jnp.zeros_like(q)), k_blk) - slopes[j] * token_dist
        s = jnp.where(valid, s, -jnp.inf)
        m = jnp.max(s, axis=1, keepdims=True)
        p = jnp.exp(s - m)
        l = jnp.sum(p, axis=1, keepdims=True)
        probs.append(p.astype(BF16))
        v_heads.append(jnp.where(in_head_k, v_blk, jnp.zeros_like(v_blk)))
        m_full = jnp.where(in_head_q, m, m_full)
        l_full = jnp.where(in_head_q, l, l_full)
    num = _dot(jnp.concatenate(probs, axis=1), jnp.concatenate(v_heads, axis=0))
    return m_full, l_full, num


def _dil_kernel(slopes_ref, a0_ref, a1_ref, a2_ref, o_ref, m_scr, l_scr, n_scr):
    slabs = DIL_GROUP_WIDTH // LANES
    qs, ks, vs = (slice(i * DIL_GROUP_WIDTH, (i + 1) * DIL_GROUP_WIDTH) for i in range(3))

    def merge(rows, m_new, l_new, n_new, first):
        for s in range(slabs):
            cols = slice(s * LANES, (s + 1) * LANES)
            m_b, l_b, n_b = m_new[:, cols], l_new[:, cols], n_new[:, cols]
            if not first:
                m_a, l_a, n_a = m_scr[s, rows, :], l_scr[s, rows, :], n_scr[s, rows, :]
                m_max = jnp.maximum(m_a, m_b)
                w_a, w_b = jnp.exp(m_a - m_max), jnp.exp(m_b - m_max)
                m_b, l_b, n_b = m_max, w_a * l_a + w_b * l_b, w_a * n_a + w_b * n_b
            m_scr[s, rows, :] = m_b
            l_scr[s, rows, :] = l_b
            n_scr[s, rows, :] = n_b

    for g, (a_ref, (_, dil)) in enumerate(zip((a0_ref, a1_ref, a2_ref), DIL_PAIRS)):
        sub_len = a_ref.shape[2]
        nblk = sub_len // DIL_BLK
        slopes = [slopes_ref[g * DIL_GROUP_HEADS + j] for j in range(DIL_GROUP_HEADS)]

        def unit(u, _, a_ref=a_ref, dil=dil, nblk=nblk, slopes=slopes, first=(g == 0)):
            r, n = u // nblk, u % nblk
            q0 = pl.multiple_of(n * DIL_BLK, DIL_BLK)
            if nblk == 1:
                k0, nk = 0, DIL_BLK
            else:
                k0, nk = pl.multiple_of(jnp.maximum(n - 1, 0) * DIL_BLK, DIL_BLK), 2 * DIL_BLK
            q = a_ref[0, r, pl.ds(q0, DIL_BLK), qs]
            k_blk = a_ref[0, r, pl.ds(k0, nk), ks]
            v_blk = a_ref[0, r, pl.ds(k0, nk), vs]
            m_new, l_new, n_new = _dil_block(q, k_blk, v_blk, q0, k0, dil, slopes)
            if dil == 1:
                rows = pl.ds(q0, DIL_BLK)
            else:
                rows = pl.ds(r + q0 * dil, DIL_BLK, stride=dil)
            merge(rows, m_new, l_new, n_new, first)
            return 0

        lax.fori_loop(0, dil * nblk, unit, 0)

    for s in range(slabs):
        o_ref[0, :, s * LANES:(s + 1) * LANES] = (n_scr[s] / l_scr[s]).astype(BF16)


def _dil_attn(slopes, a0, a1, a2):
    b = a0.shape[0]
    s = a0.shape[1] * a0.shape[2]
    slabs = DIL_GROUP_WIDTH // LANES

    def group_spec(a):
        return pl.BlockSpec((1,) + a.shape[1:], lambda bi: (bi, 0, 0, 0))

    return pl.pallas_call(
        _dil_kernel,
        grid=(b,),
        in_specs=[pl.BlockSpec(memory_space=pltpu.SMEM), group_spec(a0), group_spec(a1), group_spec(a2)],
        out_specs=pl.BlockSpec((1, s, DIL_GROUP_WIDTH), lambda bi: (bi, 0, 0)),
        out_shape=jax.ShapeDtypeStruct((b, s, DIL_GROUP_WIDTH), BF16),
        scratch_shapes=[pltpu.VMEM((slabs, s, LANES), F32)] * 3,
        compiler_params=pltpu.CompilerParams(
            dimension_semantics=("arbitrary",), vmem_limit_bytes=VMEM_LIMIT_BYTES),
        name="dil_attn",
    )(slopes, a0, a1, a2)


POST_TM = 512
FF_CHUNK = 256


def _post_kernel(x_ref, osb_ref, odl_ref, gate_ref, wsb_ref, wdl_ref, wout_ref, g2_ref, wfi_ref, wfo_ref, gf_ref,
                 o_ref, u_scr, acc_scr):
    y_sb = _dot(osb_ref[0], wsb_ref[...])
    y_dl = _dot(odl_ref[0], wdl_ref[...])
    gate_sb = jax.nn.sigmoid(gate_ref[0, :, :D_MODEL].astype(F32))
    gate_dl = jax.nn.sigmoid(gate_ref[0, :, D_MODEL:].astype(F32))
    merged = (gate_sb * y_sb + gate_dl * y_dl).astype(BF16)
    x1 = x_ref[0] + _dot(merged, wout_ref[...])
    acc_scr[...] = x1
    u_scr[...] = _rms_norm(x1, g2_ref[...]).astype(BF16)
    for c in range(D_FF // FF_CHUNK):
        lo = c * FF_CHUNK
        g_ff = _dot(u_scr[...], wfi_ref[:, lo:lo + FF_CHUNK])
        up_ff = _dot(u_scr[...], wfi_ref[:, D_FF + lo:D_FF + lo + FF_CHUNK])
        h = (g_ff * jax.nn.sigmoid(g_ff) * up_ff).astype(BF16)
        acc_scr[...] += _dot(h, wfo_ref[lo:lo + FF_CHUNK, :])
    o_ref[0] = _rms_norm(acc_scr[...], gf_ref[...])


def _post(x, o_sb, o_dl, gates, w_sb, w_dl, w_out, g2, w_fi, w_fo, gf):
    b, s, d = x.shape
    tm = min(POST_TM, s)

    def tok(width):
        return pl.BlockSpec((1, tm, width), lambda bi, i: (bi, i, 0))

    def whole(a):
        return pl.BlockSpec(a.shape, lambda bi, i: (0,) * a.ndim, pipeline_mode=pl.Buffered(1))

    return pl.pallas_call(
        _post_kernel,
        grid=(b, s // tm),
        in_specs=[tok(d), tok(SB_WIDTH), tok(DIL_GROUP_WIDTH), tok(GATE_COLS),
                  whole(w_sb), whole(w_dl), whole(w_out), whole(g2), whole(w_fi), whole(w_fo), whole(gf)],
        out_specs=tok(d),
        out_shape=jax.ShapeDtypeStruct((b, s, d), x.dtype),
        scratch_shapes=[pltpu.VMEM((tm, d), BF16), pltpu.VMEM((tm, d), F32)],
        compiler_params=pltpu.CompilerParams(
            dimension_semantics=("arbitrary", "arbitrary"), vmem_limit_bytes=VMEM_LIMIT_BYTES),
        name="post",
    )(x, o_sb, o_dl, gates, w_sb, w_dl, w_out, g2, w_fi, w_fo, gf)


def _arrange_w_in(w_in):
    o = 0
    parts = {}
    for name, width in (("q_sb", SB_WIDTH), ("k_sb", SB_WIDTH), ("v_sb", SB_WIDTH),
                        ("q_dl", DIL_WIDTH), ("k_dl", DIL_WIDTH), ("v_dl", DIL_WIDTH), ("gates", GATE_COLS)):
        parts[name] = w_in[:, o:o + width]
        o += width
    cols = [parts["q_sb"] * QK_SCALE, parts["k_sb"], parts["v_sb"]]
    for g in range(len(DIL_PAIRS)):
        grp = slice(g * DIL_GROUP_WIDTH, (g + 1) * DIL_GROUP_WIDTH)
        cols += [parts["q_dl"][:, grp] * QK_SCALE, parts["k_dl"][:, grp], parts["v_dl"][:, grp]]
    cols.append(parts["gates"])
    return jnp.concatenate(cols, axis=1).astype(BF16)


def kernel(x, norm_mix_g, w_in, w_sb_up, w_dil_up, w_out, norm_ffn_g, w_ffn_in, w_ffn_out, norm_final_g):
    assert norm_mix_g.shape[0] == 1, "one layer"
    slopes = jnp.exp2(-ALIBI_MAX_BIAS * jnp.arange(1, DIL_HEADS + 1, dtype=F32) / DIL_HEADS)
    sb_qkv, a0, a1, a2, gates = _in_proj(x, norm_mix_g[0][None, :], _arrange_w_in(w_in[0]))
    o_sb = _sb_attn(sb_qkv)
    o_dl = _dil_attn(slopes, a0, a1, a2)
    return _post(x, o_sb, o_dl, gates,
                 w_sb_up[0].astype(BF16), w_dil_up[0].astype(BF16), w_out[0].astype(BF16),
                 norm_ffn_g[0][None, :], w_ffn_in[0].astype(BF16), w_ffn_out[0].astype(BF16),
                 norm_final_g[None, :])
```
